```python
import jax, jax.numpy as jnp
from jax import lax
import numpy as np

D_MODEL = 1024
BATCH = 8
SEQ = 4096
DEPTH = 4

RET_HEADS = 4
RET_HEAD_DIM = 128
RET_WIDTH = RET_HEADS * RET_HEAD_DIM
POOL_WINDOWS = (2, 4, 8, 16)
POOL_GROUPS = len(POOL_WINDOWS)
POOL_WIDTH = D_MODEL - RET_WIDTH
POOL_GROUP_CH = POOL_WIDTH // POOL_GROUPS
POOL_MAX_W = max(POOL_WINDOWS)
MIX_WIDTH = RET_WIDTH + POOL_WIDTH
IN_WIDTH = 4 * RET_WIDTH + POOL_WIDTH
CHUNK = 128
ROPE_BASE = 10000.0
D_FF = ((8 * D_MODEL // 3 + 255) // 256) * 256
PLE_DIM = 256
EPS = 1e-6

kernel_name = "hybrid_retention_pool_swiglu_ple"


def rmsnorm(x, g):
    xf = x.astype(jnp.float32)
    y = xf * lax.rsqrt(jnp.mean(xf * xf, axis=-1, keepdims=True) + EPS)
    return (y * g.astype(jnp.float32)).astype(x.dtype)


def rotary(x, positions):
    half = RET_HEAD_DIM // 2
    inv = 1.0 / (ROPE_BASE ** (jnp.arange(half, dtype=jnp.float32) * (2.0 / RET_HEAD_DIM)))
    ang = positions.astype(jnp.float32)[..., None] * inv
    cos = jnp.cos(ang)[:, :, None, :]
    sin = jnp.sin(ang)[:, :, None, :]
    xf = x.astype(jnp.float32)
    x1, x2 = xf[..., :half], xf[..., half:]
    return jnp.concatenate([x1 * cos - x2 * sin, x2 * cos + x1 * sin], axis=-1)


def retention(q, k, v, g):
    B, S, H, Dh = q.shape
    N = S // CHUNK
    log_gamma = jnp.log1p(-(2.0 ** (-5.0 - jnp.arange(H, dtype=jnp.float32))))
    qc = (q * (Dh ** -0.5)).reshape(B, N, CHUNK, H, Dh)
    kc = k.reshape(B, N, CHUNK, H, Dh)
    vc = v.astype(jnp.float32).reshape(B, N, CHUNK, H, Dh)
    idx = jnp.arange(CHUNK, dtype=jnp.float32)
    rel = idx[:, None] - idx[None, :]
    decay_in = jnp.where(rel[None] >= 0,
                         jnp.exp(jnp.maximum(rel, 0.0)[None] * log_gamma[:, None, None]), 0.0)
    scores = jnp.einsum('bnqhd,bnkhd->bnhqk', qc, kc) * decay_in[None, None]
    inner = jnp.einsum('bnhqk,bnkhd->bnqhd', scores, vc)
    zeta = jnp.exp((CHUNK - 1.0 - idx)[:, None] * log_gamma[None, :])
    kv = jnp.einsum('bnkhd,bnkhe->nbhde', kc * zeta[None, None, :, :, None], vc)
    chunk_decay = jnp.exp(CHUNK * log_gamma)[None, :, None, None]

    def step(state, kv_i):
        return state * chunk_decay + kv_i, state

    _, state_prev = lax.scan(step, jnp.zeros((B, H, Dh, Dh), jnp.float32), kv)
    xi = jnp.exp((idx + 1.0)[:, None] * log_gamma[None, :])
    cross = jnp.einsum('bnqhd,nbhde->bnqhe', qc * xi[None, None, :, :, None], state_prev)
    y = (inner + cross).reshape(B, S, H, Dh)
    y = y * lax.rsqrt(jnp.mean(y * y, axis=-1, keepdims=True) + EPS)
    y = y.reshape(B, S, H * Dh)
    return (jax.nn.silu(g.astype(jnp.float32)) * y).astype(g.dtype)


def pool_mixer(u, w_grp, scale):
    B, S, _ = u.shape
    uf = u.astype(jnp.float32).reshape(B, S, POOL_GROUPS, POOL_GROUP_CH)
    cs = jnp.cumsum(uf, axis=1)
    csp = jnp.pad(cs, ((0, 0), (POOL_MAX_W, 0), (0, 0), (0, 0)))
    t = jnp.arange(S)
    means = []
    for gi, w in enumerate(POOL_WINDOWS):
        win = csp[:, POOL_MAX_W:, gi] - csp[:, POOL_MAX_W - w:POOL_MAX_W - w + S, gi]
        cnt = jnp.minimum(t + 1, w).astype(jnp.float32)[None, :, None]
        means.append(win / cnt)
    pooled = jnp.stack(means, axis=2) - uf
    y = jnp.einsum('bsgc,gcd->bsgd', pooled, w_grp.astype(jnp.float32)).reshape(B, S, POOL_WIDTH)
    return (y * scale.astype(jnp.float32)).astype(u.dtype)


def setup_inputs(seed: int = 0) -> dict:
    key = jax.random.key(seed)
    ks = jax.random.split(key, 16)
    f32 = jnp.float32

    def nrm(k, shape, fan_in):
        return jax.random.normal(k, shape, f32) * (fan_in ** -0.5)

    def gain(k, shape):
        return 1.0 + 0.02 * jax.random.normal(k, shape, f32)

    x = jax.random.normal(ks[0], (BATCH, SEQ, D_MODEL), f32)
    p = jax.random.normal(ks[1], (DEPTH, BATCH, SEQ, PLE_DIM), f32)
    positions = jnp.broadcast_to(jnp.arange(SEQ, dtype=jnp.int32), (BATCH, SEQ))
    return {
        "x": x,
        "p": p,
        "positions": positions,
        "g_mix": gain(ks[2], (DEPTH, D_MODEL)),
        "w_in": nrm(ks[3], (DEPTH, D_MODEL, IN_WIDTH), D_MODEL),
        "pool_w": nrm(ks[4], (DEPTH, POOL_GROUPS, POOL_GROUP_CH, POOL_GROUP_CH), POOL_GROUP_CH),
        "pool_scale": 0.5 + 0.1 * jax.random.normal(ks[5], (DEPTH, POOL_WIDTH), f32),
        "w_out": nrm(ks[6], (DEPTH, MIX_WIDTH, D_MODEL), MIX_WIDTH),
        "g_ffn": gain(ks[7], (DEPTH, D_MODEL)),
        "w_gate": nrm(ks[8], (DEPTH, D_MODEL, D_FF), D_MODEL),
        "w_up": nrm(ks[9], (DEPTH, D_MODEL, D_FF), D_MODEL),
        "w_down": nrm(ks[10], (DEPTH, D_FF, D_MODEL), D_FF),
        "g_ple": gain(ks[11], (DEPTH, D_MODEL)),
        "w_ple_gate": nrm(ks[12], (DEPTH, D_MODEL, D_MODEL), D_MODEL),
        "w_ple_proj": nrm(ks[13], (DEPTH, PLE_DIM, D_MODEL), PLE_DIM),
        "g_final": gain(ks[14], (D_MODEL,)),
    }


def reference(x, p, positions, g_mix, w_in, pool_w, pool_scale, w_out, g_ffn,
              w_gate, w_up, w_down, g_ple, w_ple_gate, w_ple_proj, g_final):
    B, S, _ = x.shape
    for i in range(DEPTH):
        h = rmsnorm(x, g_mix[i])
        z = h @ w_in[i]
        q, k, v, g, u = jnp.split(z, [RET_WIDTH, 2 * RET_WIDTH, 3 * RET_WIDTH, 4 * RET_WIDTH], axis=-1)
        q = rotary(q.reshape(B, S, RET_HEADS, RET_HEAD_DIM), positions)
        k = rotary(k.reshape(B, S, RET_HEADS, RET_HEAD_DIM), positions)
        v = v.reshape(B, S, RET_HEADS, RET_HEAD_DIM)
        y_ret = retention(q, k, v, g)
        y_pool = pool_mixer(u, pool_w[i], pool_scale[i])
        mix = jnp.concatenate([y_ret, y_pool], axis=-1)
        x = x + (mix @ w_out[i]).astype(x.dtype)
        h2 = rmsnorm(x, g_ffn[i])
        ff = (jax.nn.silu(h2 @ w_gate[i]) * (h2 @ w_up[i])) @ w_down[i]
        x = x + ff.astype(x.dtype)
        gate = jax.nn.sigmoid((rmsnorm(x, g_ple[i]) @ w_ple_gate[i]).astype(jnp.float32))
        e = (p[i] @ w_ple_proj[i]).astype(jnp.float32)
        x = x + (gate * e).astype(x.dtype)
    return rmsnorm(x, g_final)
```

```python
import functools

import jax
import jax.numpy as jnp
from jax import lax
from jax.experimental import pallas as pl
from jax.experimental.pallas import tpu as pltpu

D_MODEL = 1024
DEPTH = 4
RET_HEADS = 4
HEAD_DIM = 128
RET_WIDTH = RET_HEADS * HEAD_DIM
POOL_WINDOWS = (2, 4, 8, 16)
POOL_GROUPS = len(POOL_WINDOWS)
GROUP_CH = 128
POOL_WIDTH = POOL_GROUPS * GROUP_CH
IN_WIDTH = 4 * RET_WIDTH + POOL_WIDTH
CHUNK = 128
ROPE_BASE = 10000.0
D_FF = 2816
PLE_DIM = 256
EPS = 1e-6

POOL_HALO = 16
MIX_TILE = 512
FFN_TILE = 512
FF_CHUNKS = (1024, 1024, 768)
VMEM_LIMIT_BYTES = 56 * 1024 * 1024

F32 = jnp.float32
BF16 = jnp.bfloat16


def _rmsnorm(x, g):
    ms = jnp.mean(x * x, axis=-1, keepdims=True)
    return (x * lax.rsqrt(ms + EPS)) * g


def _rope_kernel(pos_ref, inv_ref, sign_ref, cos_ref, sin_ref):
    ang = pos_ref[...].astype(F32) * inv_ref[...]
    cos_ref[...] = jnp.cos(ang)
    sin_ref[...] = jnp.sin(ang) * sign_ref[...]


def _rope_tables(positions):
    b, s = positions.shape
    half = HEAD_DIM // 2
    inv = 1.0 / (ROPE_BASE ** (jnp.arange(half, dtype=F32) * (2.0 / HEAD_DIM)))
    inv_full = jnp.concatenate([inv, inv])[None, :]
    sign = jnp.concatenate([-jnp.ones((half,), F32), jnp.ones((half,), F32)])[None, :]
    pos = jnp.broadcast_to(positions.reshape(b * s, 1), (b * s, HEAD_DIM))
    tile = 2048
    spec = pl.BlockSpec((tile, HEAD_DIM), lambda i: (i, 0))
    row = pl.BlockSpec((1, HEAD_DIM), lambda i: (0, 0))
    cos, sin = pl.pallas_call(
        _rope_kernel,
        grid=(b * s // tile,),
        in_specs=[spec, row, row],
        out_specs=[spec, spec],
        out_shape=[jax.ShapeDtypeStruct((b * s, HEAD_DIM), F32)] * 2,
        name="rope_tables",
    )(pos, inv_full, sign)
    return cos.reshape(b, s, HEAD_DIM), sin.reshape(b, s, HEAD_DIM)


def _retention_tables():
    h = jnp.arange(RET_HEADS, dtype=F32)
    log_gamma = jnp.log1p(-(2.0 ** (-5.0 - h)))
    idx = jnp.arange(CHUNK, dtype=F32)
    rel = idx[:, None] - idx[None, :]
    decay_in = jnp.where(rel[None] >= 0,
                         jnp.exp(jnp.maximum(rel, 0.0)[None] * log_gamma[:, None, None]), 0.0)
    scale = HEAD_DIM ** -0.5
    zeta = jnp.exp((CHUNK - 1.0 - idx)[None, :] * log_gamma[:, None])
    xi = jnp.exp((idx + 1.0)[None, :] * log_gamma[:, None])
    chunk_decay = jnp.exp(CHUNK * log_gamma)
    lanes = (RET_HEADS, CHUNK, HEAD_DIM)
    return (decay_in * scale,
            jnp.broadcast_to(zeta[:, :, None], lanes),
            jnp.broadcast_to((xi * scale)[:, :, None], lanes),
            jnp.broadcast_to(chunk_decay[:, None, None], (RET_HEADS, 1, HEAD_DIM)))


def _mix_kernel(x_ref, cos_ref, sin_ref, g_ref, win_ref, pw_ref, ps_ref, wout_ref,
                dec_ref, zeta_ref, xi_ref, cd_ref, o_ref, state_ref, halo_ref, mix_ref):
    ts = x_ref.shape[0]
    j = pl.program_id(1)

    @pl.when(j == 0)
    def _():
        state_ref[...] = jnp.zeros_like(state_ref)
        halo_ref[...] = jnp.zeros_like(halo_ref)

    x = x_ref[...]
    h = _rmsnorm(x, g_ref[...]).astype(BF16)
    z = jnp.dot(h, win_ref[...], preferred_element_type=F32)
    cos = cos_ref[...]
    sin = sin_ref[...]

    for hd in range(RET_HEADS):
        c0 = hd * HEAD_DIM
        q = z[:, c0:c0 + HEAD_DIM]
        k = z[:, RET_WIDTH + c0:RET_WIDTH + c0 + HEAD_DIM]
        v = z[:, 2 * RET_WIDTH + c0:2 * RET_WIDTH + c0 + HEAD_DIM].astype(BF16)
        g = z[:, 3 * RET_WIDTH + c0:3 * RET_WIDTH + c0 + HEAD_DIM]
        qr = q * cos + pltpu.roll(q, HEAD_DIM // 2, 1) * sin
        kr = k * cos + pltpu.roll(k, HEAD_DIM // 2, 1) * sin
        dec = dec_ref[hd]
        zeta = zeta_ref[hd]
        xi = xi_ref[hd]
        cd = cd_ref[hd]
        st = state_ref[hd]
        for c in range(ts // CHUNK):
            r0 = c * CHUNK
            qc = qr[r0:r0 + CHUNK]
            kc = kr[r0:r0 + CHUNK]
            vc = v[r0:r0 + CHUNK]
            s = lax.dot_general(qc.astype(BF16), kc.astype(BF16), (((1,), (1,)), ((), ())),
                                preferred_element_type=F32) * dec
            lhs = jnp.concatenate([s.astype(BF16), (qc * xi).astype(BF16)], axis=1)
            rhs = jnp.concatenate([vc, st.astype(BF16)], axis=0)
            y = jnp.dot(lhs, rhs, preferred_element_type=F32)
            kv = lax.dot_general((kc * zeta).astype(BF16), vc, (((0,), (0,)), ((), ())),
                                 preferred_element_type=F32)
            st = st * cd + kv
            y = y * lax.rsqrt(jnp.mean(y * y, axis=-1, keepdims=True) + EPS)
            gc = g[r0:r0 + CHUNK]
            mix_ref[r0:r0 + CHUNK, c0:c0 + HEAD_DIM] = ((gc * jax.nn.sigmoid(gc)) * y).astype(BF16)
        state_ref[hd] = st

    u = z[:, 4 * RET_WIDTH:]
    ue = jnp.concatenate([halo_ref[...], u], axis=0)
    halo_ref[...] = u[ts - POOL_HALO:]
    tpos = (lax.broadcasted_iota(jnp.int32, (ts, GROUP_CH), 0) + (j * ts + 1)).astype(F32)
    acc = ue
    width = 1
    for gi, w in enumerate(POOL_WINDOWS):
        while width < w:
            acc = acc + pltpu.roll(acc, width, 0)
            width *= 2
        win = acc[POOL_HALO:, :GROUP_CH]
        acc = acc[:, GROUP_CH:]
        ug = u[:, gi * GROUP_CH:(gi + 1) * GROUP_CH]
        pooled = win / jnp.minimum(tpos, float(w)) - ug
        yp = jnp.dot(pooled.astype(BF16), pw_ref[gi], preferred_element_type=F32)
        yp = yp * ps_ref[:, gi * GROUP_CH:(gi + 1) * GROUP_CH]
        mix_ref[:, RET_WIDTH + gi * GROUP_CH:RET_WIDTH + (gi + 1) * GROUP_CH] = yp.astype(BF16)

    o_ref[...] = x + jnp.dot(mix_ref[...], wout_ref[...], preferred_element_type=F32)


def _const_spec(shape, layer, grid_rank):
    zeros = (0,) * len(shape)
    if grid_rank == 1:
        index_map = lambda i: (layer,) + zeros
    else:
        index_map = lambda i, j: (layer,) + zeros
    return pl.BlockSpec((None,) + tuple(shape), index_map, pipeline_mode=pl.Buffered(1))


def _table_spec(shape):
    return pl.BlockSpec(tuple(shape), lambda i, j: (0,) * len(shape), pipeline_mode=pl.Buffered(1))


def _mix_layer(x, cos, sin, tables, g_mix, w_in, pool_w, pool_scale, w_out, layer):
    b, s, d = x.shape
    ts = MIX_TILE
    act = lambda width: pl.BlockSpec((None, ts, width), lambda i, j: (i, j, 0))
    return pl.pallas_call(
        _mix_kernel,
        grid=(b, s // ts),
        in_specs=[
            act(d), act(HEAD_DIM), act(HEAD_DIM),
            _const_spec((1, d), layer, 2),
            _const_spec((d, IN_WIDTH), layer, 2),
            _const_spec((POOL_GROUPS, GROUP_CH, GROUP_CH), layer, 2),
            _const_spec((1, POOL_WIDTH), layer, 2),
            _const_spec((d, d), layer, 2),
            _table_spec((RET_HEADS, CHUNK, CHUNK)),
            _table_spec((RET_HEADS, CHUNK, HEAD_DIM)),
            _table_spec((RET_HEADS, CHUNK, HEAD_DIM)),
            _table_spec((RET_HEADS, 1, HEAD_DIM)),
        ],
        out_specs=act(d),
        out_shape=jax.ShapeDtypeStruct(x.shape, x.dtype),
        scratch_shapes=[
            pltpu.VMEM((RET_HEADS, HEAD_DIM, HEAD_DIM), F32),
            pltpu.VMEM((POOL_HALO, POOL_WIDTH), F32),
            pltpu.VMEM((ts, d), BF16),
        ],
        compiler_params=pltpu.CompilerParams(
            dimension_semantics=("arbitrary", "arbitrary"), vmem_limit_bytes=VMEM_LIMIT_BYTES),
        name=f"mix_layer{layer}",
    )(x, cos, sin, g_mix, w_in, pool_w, pool_scale, w_out, *tables)


def _ffn_kernel(x_ref, p_ref, gffn_ref, wg_ref, wu_ref, wd_ref, gple_ref, wpg_ref, wpp_ref, *rest, final):
    o_ref = rest[-1]
    x = x_ref[...]
    h = _rmsnorm(x, gffn_ref[...]).astype(BF16)
    c0 = 0
    for cw in FF_CHUNKS:
        gate = jnp.dot(h, wg_ref[:, c0:c0 + cw], preferred_element_type=F32)
        up = jnp.dot(h, wu_ref[:, c0:c0 + cw], preferred_element_type=F32)
        a = ((gate * jax.nn.sigmoid(gate)) * up).astype(BF16)
        x = x + jnp.dot(a, wd_ref[c0:c0 + cw, :], preferred_element_type=F32)
        c0 += cw
    hp = _rmsnorm(x, gple_ref[...]).astype(BF16)
    gate = jax.nn.sigmoid(jnp.dot(hp, wpg_ref[...], preferred_element_type=F32))
    e = jnp.dot(p_ref[...].astype(BF16), wpp_ref[...], preferred_element_type=F32)
    x = x + gate * e
    if final:
        x = _rmsnorm(x, rest[0][...])
    o_ref[...] = x


def _ffn_layer(x, p, g_ffn, w_gate, w_up, w_down, g_ple, w_ple_gate, w_ple_proj, g_final, layer):
    t, d = x.shape
    tm = FFN_TILE
    final = layer == DEPTH - 1
    in_specs = [
        pl.BlockSpec((tm, d), lambda i: (i, 0)),
        pl.BlockSpec((None, tm, PLE_DIM), lambda i: (layer, i, 0)),
        _const_spec((1, d), layer, 1),
        _const_spec((d, D_FF), layer, 1),
        _const_spec((d, D_FF), layer, 1),
        _const_spec((D_FF, d), layer, 1),
        _const_spec((1, d), layer, 1),
        _const_spec((d, d), layer, 1),
        _const_spec((PLE_DIM, d), layer, 1),
    ]
    args = [x, p, g_ffn, w_gate, w_up, w_down, g_ple, w_ple_gate, w_ple_proj]
    if final:
        in_specs.append(pl.BlockSpec((1, d), lambda i: (0, 0), pipeline_mode=pl.Buffered(1)))
        args.append(g_final)
    return pl.pallas_call(
        functools.partial(_ffn_kernel, final=final),
        grid=(t // tm,),
        in_specs=in_specs,
        out_specs=pl.BlockSpec((tm, d), lambda i: (i, 0)),
        out_shape=jax.ShapeDtypeStruct(x.shape, x.dtype),
        compiler_params=pltpu.CompilerParams(
            dimension_semantics=("arbitrary",), vmem_limit_bytes=VMEM_LIMIT_BYTES),
        name=f"ffn_layer{layer}",
    )(*args)


def kernel(x, p, positions, g_mix, w_in, pool_w, pool_scale, w_out, g_ffn, w_gate, w_up, w_down,
           g_ple, w_ple_gate, w_ple_proj, g_final):
    b, s, d = x.shape
    assert (d, s % MIX_TILE, (b * s) % FFN_TILE) == (D_MODEL, 0, 0)
    cos, sin = _rope_tables(positions)
    tables = _retention_tables()
    bf = lambda w: w.astype(BF16)
    w_in, pool_w, w_out, w_gate, w_up, w_down, w_ple_gate, w_ple_proj = map(
        bf, (w_in, pool_w, w_out, w_gate, w_up, w_down, w_ple_gate, w_ple_proj))
    row = lambda g: g.reshape(g.shape[0], 1, g.shape[1])
    g_mix, pool_scale, g_ffn, g_ple = map(row, (g_mix, pool_scale, g_ffn, g_ple))
    p = p.reshape(DEPTH, b * s, PLE_DIM)
    for layer in range(DEPTH):
        x = _mix_layer(x, cos, sin, tables, g_mix, w_in, pool_w, pool_scale, w_out, layer)
        x = _ffn_layer(x.reshape(b * s, d), p, g_ffn, w_gate, w_up, w_down, g_ple, w_ple_gate,
                       w_ple_proj, g_final.reshape(1, d), layer).reshape(b, s, d)
    return x
```

```python
import functools

import jax
import jax.numpy as jnp
from jax import lax
from jax.experimental import pallas as pl
from jax.experimental.pallas import tpu as pltpu

D_MODEL = 1024
DEPTH = 4
RET_HEADS = 4
HEAD_DIM = 128
RET_WIDTH = RET_HEADS * HEAD_DIM
POOL_WINDOWS = (2, 4, 8, 16)
POOL_GROUPS = len(POOL_WINDOWS)
GROUP_CH = 128
POOL_WIDTH = POOL_GROUPS * GROUP_CH
IN_WIDTH = 4 * RET_WIDTH + POOL_WIDTH
CHUNK = 128
ROPE_BASE = 10000.0
D_FF = 2816
PLE_DIM = 256
EPS = 1e-6

POOL_HALO = 16
MIX_TILE = 512
FFN_TILE = 1024
MIX_ROWS = 2
FF_CHUNKS = (1024, 1024, 768)
VMEM_LIMIT_BYTES = 56 * 1024 * 1024

F32 = jnp.float32
BF16 = jnp.bfloat16


def _rmsnorm(x, g):
    ms = jnp.mean(x * x, axis=-1, keepdims=True)
    return (x * lax.rsqrt(ms + EPS)) * g


def _rope_kernel(pos_ref, inv_ref, sign_ref, cos_ref, sin_ref):
    ang = pos_ref[...].astype(F32) * inv_ref[...]
    cos_ref[...] = jnp.cos(ang)
    sin_ref[...] = jnp.sin(ang) * sign_ref[...]


def _rope_tables(positions):
    b, s = positions.shape
    half = HEAD_DIM // 2
    inv = 1.0 / (ROPE_BASE ** (jnp.arange(half, dtype=F32) * (2.0 / HEAD_DIM)))
    inv_full = jnp.concatenate([inv, inv])[None, :]
    sign = jnp.concatenate([-jnp.ones((half,), F32), jnp.ones((half,), F32)])[None, :]
    pos = jnp.broadcast_to(positions.reshape(b * s, 1), (b * s, HEAD_DIM))
    tile = 2048
    spec = pl.BlockSpec((tile, HEAD_DIM), lambda i: (i, 0))
    row = pl.BlockSpec((1, HEAD_DIM), lambda i: (0, 0))
    cos, sin = pl.pallas_call(
        _rope_kernel,
        grid=(b * s // tile,),
        in_specs=[spec, row, row],
        out_specs=[spec, spec],
        out_shape=[jax.ShapeDtypeStruct((b * s, HEAD_DIM), F32)] * 2,
        name="rope_tables",
    )(pos, inv_full, sign)
    return cos.reshape(b, s, HEAD_DIM), sin.reshape(b, s, HEAD_DIM)


def _retention_tables():
    h = jnp.arange(RET_HEADS, dtype=F32)
    log_gamma = jnp.log1p(-(2.0 ** (-5.0 - h)))
    idx = jnp.arange(CHUNK, dtype=F32)
    rel = idx[:, None] - idx[None, :]
    decay_in = jnp.where(rel[None] >= 0,
                         jnp.exp(jnp.maximum(rel, 0.0)[None] * log_gamma[:, None, None]), 0.0)
    scale = HEAD_DIM ** -0.5
    zeta = jnp.exp((CHUNK - 1.0 - idx)[None, :] * log_gamma[:, None])
    xi = jnp.exp((idx + 1.0)[None, :] * log_gamma[:, None])
    chunk_decay = jnp.exp(CHUNK * log_gamma)
    lanes = (RET_HEADS, CHUNK, HEAD_DIM)
    return (decay_in * scale,
            jnp.broadcast_to(zeta[:, :, None], lanes),
            jnp.broadcast_to((xi * scale)[:, :, None], lanes),
            jnp.broadcast_to(chunk_decay[:, None, None], (RET_HEADS, 1, HEAD_DIM)))


def _rotary(x, cos, sin):
    return x * cos + pltpu.roll(x, HEAD_DIM // 2, 1) * sin


def _chunks(x):
    return [x[r0:r0 + CHUNK] for r0 in range(0, x.shape[0], CHUNK)]


def _pool_groups(u, j, pw_ref, ps_ref, halo_ref, mix_ref):
    ts = u.shape[0]
    ue = jnp.concatenate([halo_ref[...], u], axis=0)
    halo_ref[...] = u[ts - POOL_HALO:]
    tpos = (lax.broadcasted_iota(jnp.int32, (ts, GROUP_CH), 0) + (j * ts + 1)).astype(F32)
    acc = ue
    width = 1
    for gi, w in enumerate(POOL_WINDOWS):
        while width < w:
            acc = acc + pltpu.roll(acc, width, 0)
            width *= 2
        win = acc[POOL_HALO:, :GROUP_CH]
        acc = acc[:, GROUP_CH:]
        ug = u[:, gi * GROUP_CH:(gi + 1) * GROUP_CH]
        pooled = win / jnp.minimum(tpos, float(w)) - ug
        yp = jnp.dot(pooled.astype(BF16), pw_ref[gi], preferred_element_type=F32)
        yp = yp * ps_ref[:, gi * GROUP_CH:(gi + 1) * GROUP_CH]
        mix_ref[:, RET_WIDTH + gi * GROUP_CH:RET_WIDTH + (gi + 1) * GROUP_CH] = yp.astype(BF16)


def _mix_stream(x, cos, sin, j, g_ref, win_ref, pw_ref, ps_ref, wout_ref,
                dec_ref, zeta_ref, xi_ref, cd_ref, state_ref, halo_ref, mix_ref):
    h = _rmsnorm(x, g_ref[...]).astype(BF16)
    half = 2 * HEAD_DIM
    proj = lambda i: jnp.dot(h, win_ref[:, i * half:(i + 1) * half], preferred_element_type=F32)
    nt_dims = (((1,), (1,)), ((), ()))
    tn_dims = (((0,), (0,)), ((), ()))
    qk = proj(0)
    vg = proj(1)
    qr = _rotary(qk[:, :HEAD_DIM], cos, sin)
    kr = _rotary(qk[:, HEAD_DIM:], cos, sin)
    for hd in range(RET_HEADS):
        dec, zeta, xi, cd = dec_ref[hd], zeta_ref[hd], xi_ref[hd], cd_ref[hd]
        qs, ks = _chunks(qr), _chunks(kr)
        vs = _chunks(vg[:, :HEAD_DIM].astype(BF16))
        gs = _chunks(vg[:, HEAD_DIM:])
        scores = [lax.dot_general(q.astype(BF16), k.astype(BF16), nt_dims, preferred_element_type=F32)
                  for q, k in zip(qs, ks)]
        kvs = [lax.dot_general((k * zeta).astype(BF16), v, tn_dims, preferred_element_type=F32)
               for k, v in zip(ks, vs)]
        qk_next = proj(2 * hd + 2)
        if hd + 1 < RET_HEADS:
            qr = _rotary(qk_next[:, :HEAD_DIM], cos, sin)
            kr = _rotary(qk_next[:, HEAD_DIM:], cos, sin)
        st = state_ref[hd]
        ys = []
        for q, v, s, kv in zip(qs, vs, scores, kvs):
            lhs = jnp.concatenate([(s * dec).astype(BF16), (q * xi).astype(BF16)], axis=1)
            rhs = jnp.concatenate([v, st.astype(BF16)], axis=0)
            ys.append(jnp.dot(lhs, rhs, preferred_element_type=F32))
            st = st * cd + kv
        state_ref[hd] = st
        vg_next = proj(2 * hd + 3)
        for c, (y, g) in enumerate(zip(ys, gs)):
            y = y * lax.rsqrt(jnp.mean(y * y, axis=-1, keepdims=True) + EPS)
            mix_ref[c * CHUNK:(c + 1) * CHUNK, hd * HEAD_DIM:(hd + 1) * HEAD_DIM] = (
                (g * jax.nn.sigmoid(g)) * y).astype(BF16)
        vg = vg_next
    u = jnp.concatenate([qk_next, vg_next], axis=1)
    y = x + jnp.dot(mix_ref[:, :RET_WIDTH], wout_ref[:RET_WIDTH, :], preferred_element_type=F32)
    _pool_groups(u, j, pw_ref, ps_ref, halo_ref, mix_ref)
    return y + jnp.dot(mix_ref[:, RET_WIDTH:], wout_ref[RET_WIDTH:, :], preferred_element_type=F32)


def _mix_kernel(x_ref, cos_ref, sin_ref, g_ref, win_ref, pw_ref, ps_ref, wout_ref,
                dec_ref, zeta_ref, xi_ref, cd_ref, o_ref, state_ref, halo_ref, mix_ref):
    j = pl.program_id(1)

    @pl.when(j == 0)
    def _():
        state_ref[...] = jnp.zeros_like(state_ref)
        halo_ref[...] = jnp.zeros_like(halo_ref)

    for r in range(x_ref.shape[0]):
        o_ref[r] = _mix_stream(x_ref[r], cos_ref[r], sin_ref[r], j, g_ref, win_ref, pw_ref, ps_ref, wout_ref,
                               dec_ref, zeta_ref, xi_ref, cd_ref,
                               state_ref.at[r], halo_ref.at[r], mix_ref.at[r])


def _const_spec(shape, layer, grid_rank):
    zeros = (0,) * len(shape)
    if grid_rank == 1:
        index_map = lambda i: (layer,) + zeros
    else:
        index_map = lambda i, j: (layer,) + zeros
    return pl.BlockSpec((None,) + tuple(shape), index_map, pipeline_mode=pl.Buffered(1))


def _table_spec(shape):
    return pl.BlockSpec(tuple(shape), lambda i, j: (0,) * len(shape), pipeline_mode=pl.Buffered(1))


def _mix_layer(x, cos, sin, tables, g_mix, w_in, pool_w, pool_scale, w_out, layer):
    b, s, d = x.shape
    ts = MIX_TILE
    rows = MIX_ROWS
    act = lambda width: pl.BlockSpec((rows, ts, width), lambda i, j: (i, j, 0))
    return pl.pallas_call(
        _mix_kernel,
        grid=(b // rows, s // ts),
        in_specs=[
            act(d), act(HEAD_DIM), act(HEAD_DIM),
            _const_spec((1, d), layer, 2),
            _const_spec((d, IN_WIDTH), layer, 2),
            _const_spec((POOL_GROUPS, GROUP_CH, GROUP_CH), layer, 2),
            _const_spec((1, POOL_WIDTH), layer, 2),
            _const_spec((d, d), layer, 2),
            _table_spec((RET_HEADS, CHUNK, CHUNK)),
            _table_spec((RET_HEADS, CHUNK, HEAD_DIM)),
            _table_spec((RET_HEADS, CHUNK, HEAD_DIM)),
            _table_spec((RET_HEADS, 1, HEAD_DIM)),
        ],
        out_specs=act(d),
        out_shape=jax.ShapeDtypeStruct(x.shape, x.dtype),
        scratch_shapes=[
            pltpu.VMEM((rows, RET_HEADS, HEAD_DIM, HEAD_DIM), F32),
            pltpu.VMEM((rows, POOL_HALO, POOL_WIDTH), F32),
            pltpu.VMEM((rows, ts, d), BF16),
        ],
        compiler_params=pltpu.CompilerParams(
            dimension_semantics=("arbitrary", "arbitrary"), vmem_limit_bytes=VMEM_LIMIT_BYTES),
        name=f"mix_layer{layer}",
    )(x, cos, sin, g_mix, w_in, pool_w, pool_scale, w_out, *tables)


def _ffn_kernel(x_ref, p_ref, gffn_ref, wg_ref, wu_ref, wd_ref, gple_ref, wpg_ref, wpp_ref, *rest, final):
    o_ref = rest[-1]
    x = x_ref[...]
    h = _rmsnorm(x, gffn_ref[...]).astype(BF16)
    c0 = 0
    for cw in FF_CHUNKS:
        gate = jnp.dot(h, wg_ref[:, c0:c0 + cw], preferred_element_type=F32)
        up = jnp.dot(h, wu_ref[:, c0:c0 + cw], preferred_element_type=F32)
        a = ((gate * jax.nn.sigmoid(gate)) * up).astype(BF16)
        x = x + jnp.dot(a, wd_ref[c0:c0 + cw, :], preferred_element_type=F32)
        c0 += cw
    hp = _rmsnorm(x, gple_ref[...]).astype(BF16)
    gate = jax.nn.sigmoid(jnp.dot(hp, wpg_ref[...], preferred_element_type=F32))
    e = jnp.dot(p_ref[...].astype(BF16), wpp_ref[...], preferred_element_type=F32)
    x = x + gate * e
    if final:
        x = _rmsnorm(x, rest[0][...])
    o_ref[...] = x


def _ffn_layer(x, p, g_ffn, w_gate, w_up, w_down, g_ple, w_ple_gate, w_ple_proj, g_final, layer):
    t, d = x.shape
    tm = FFN_TILE
    final = layer == DEPTH - 1
    in_specs = [
        pl.BlockSpec((tm, d), lambda i: (i, 0)),
        pl.BlockSpec((None, tm, PLE_DIM), lambda i: (layer, i, 0)),
        _const_spec((1, d), layer, 1),
        _const_spec((d, D_FF), layer, 1),
        _const_spec((d, D_FF), layer, 1),
        _const_spec((D_FF, d), layer, 1),
        _const_spec((1, d), layer, 1),
        _const_spec((d, d), layer, 1),
        _const_spec((PLE_DIM, d), layer, 1),
    ]
    args = [x, p, g_ffn, w_gate, w_up, w_down, g_ple, w_ple_gate, w_ple_proj]
    if final:
        in_specs.append(pl.BlockSpec((1, d), lambda i: (0, 0), pipeline_mode=pl.Buffered(1)))
        args.append(g_final)
    return pl.pallas_call(
        functools.partial(_ffn_kernel, final=final),
        grid=(t // tm,),
        in_specs=in_specs,
        out_specs=pl.BlockSpec((tm, d), lambda i: (i, 0)),
        out_shape=jax.ShapeDtypeStruct(x.shape, x.dtype),
        compiler_params=pltpu.CompilerParams(
            dimension_semantics=("arbitrary",), vmem_limit_bytes=VMEM_LIMIT_BYTES),
        name=f"ffn_layer{layer}",
    )(*args)


def _group_heads(w_in):
    lead = w_in.shape[:-1]
    ret = w_in[..., :4 * RET_WIDTH].reshape(*lead, 4, RET_HEADS, HEAD_DIM)
    ret = jnp.swapaxes(ret, -3, -2).reshape(*lead, 4 * RET_WIDTH)
    return jnp.concatenate([ret, w_in[..., 4 * RET_WIDTH:]], axis=-1)


def kernel(x, p, positions, g_mix, w_in, pool_w, pool_scale, w_out, g_ffn, w_gate, w_up, w_down,
           g_ple, w_ple_gate, w_ple_proj, g_final):
    b, s, d = x.shape
    assert (d, s % MIX_TILE, b % MIX_ROWS, (b * s) % FFN_TILE) == (D_MODEL, 0, 0, 0)
    cos, sin = _rope_tables(positions)
    tables = _retention_tables()
    bf = lambda w: w.astype(BF16)
    w_in = _group_heads(w_in)
    w_in, pool_w, w_out, w_gate, w_up, w_down, w_ple_gate, w_ple_proj = map(
        bf, (w_in, pool_w, w_out, w_gate, w_up, w_down, w_ple_gate, w_ple_proj))
    row = lambda g: g.reshape(g.shape[0], 1, g.shape[1])
    g_mix, pool_scale, g_ffn, g_ple = map(row, (g_mix, pool_scale, g_ffn, g_ple))
    p = p.reshape(DEPTH, b * s, PLE_DIM)
    for layer in range(DEPTH):
        x = _mix_layer(x, cos, sin, tables, g_mix, w_in, pool_w, pool_scale, w_out, layer)
        x = _ffn_layer(x.reshape(b * s, d), p, g_ffn, w_gate, w_up, w_down, g_ple, w_ple_gate,
                       w_ple_proj, g_final.reshape(1, d), layer).reshape(b, s, d)
    return x
```

```python
import functools

import jax
import jax.numpy as jnp
from jax import lax
from jax.experimental import pallas as pl
from jax.experimental.pallas import tpu as pltpu

D_MODEL = 1024
DEPTH = 4
RET_HEADS = 4
HEAD_DIM = 128
RET_WIDTH = RET_HEADS * HEAD_DIM
POOL_WINDOWS = (2, 4, 8, 16)
POOL_GROUPS = len(POOL_WINDOWS)
GROUP_CH = 128
POOL_WIDTH = POOL_GROUPS * GROUP_CH
IN_WIDTH = 4 * RET_WIDTH + POOL_WIDTH
CHUNK = 128
ROPE_BASE = 10000.0
D_FF = 2816
PLE_DIM = 256
EPS = 1e-6

POOL_HALO = 16
MIX_TILE = 512
FFN_TILE = 1024
MIX_ROWS = 2
FF_CHUNKS = (1024, 1024, 768)
VMEM_LIMIT_BYTES = 56 * 1024 * 1024

F32 = jnp.float32
BF16 = jnp.bfloat16


def _rmsnorm(x, g):
    ms = jnp.mean(x * x, axis=-1, keepdims=True)
    return (x * lax.rsqrt(ms + EPS)) * g


def _rope_kernel(pos_ref, inv_ref, sign_ref, cos_ref, sin_ref):
    ang = pos_ref[...].astype(F32) * inv_ref[...]
    cos_ref[...] = jnp.cos(ang)
    sin_ref[...] = jnp.sin(ang) * sign_ref[...]


def _rope_tables(positions):
    b, s = positions.shape
    half = HEAD_DIM // 2
    inv = 1.0 / (ROPE_BASE ** (jnp.arange(half, dtype=F32) * (2.0 / HEAD_DIM)))
    inv_full = jnp.concatenate([inv, inv])[None, :]
    sign = jnp.concatenate([-jnp.ones((half,), F32), jnp.ones((half,), F32)])[None, :]
    pos = jnp.broadcast_to(positions.reshape(b * s, 1), (b * s, HEAD_DIM))
    tile = 2048
    spec = pl.BlockSpec((tile, HEAD_DIM), lambda i: (i, 0))
    row = pl.BlockSpec((1, HEAD_DIM), lambda i: (0, 0))
    cos, sin = pl.pallas_call(
        _rope_kernel,
        grid=(b * s // tile,),
        in_specs=[spec, row, row],
        out_specs=[spec, spec],
        out_shape=[jax.ShapeDtypeStruct((b * s, HEAD_DIM), F32)] * 2,
        name="rope_tables",
    )(pos, inv_full, sign)
    return cos.reshape(b, s, HEAD_DIM), sin.reshape(b, s, HEAD_DIM)


def _retention_tables():
    h = jnp.arange(RET_HEADS, dtype=F32)
    log_gamma = jnp.log1p(-(2.0 ** (-5.0 - h)))
    idx = jnp.arange(CHUNK, dtype=F32)
    rel = idx[:, None] - idx[None, :]
    decay_in = jnp.where(rel[None] >= 0,
                         jnp.exp(jnp.maximum(rel, 0.0)[None] * log_gamma[:, None, None]), 0.0)
    scale = HEAD_DIM ** -0.5
    zeta = jnp.exp((CHUNK - 1.0 - idx)[None, :] * log_gamma[:, None])
    xi = jnp.exp((idx + 1.0)[None, :] * log_gamma[:, None])
    chunk_decay = jnp.exp(CHUNK * log_gamma)
    lanes = (RET_HEADS, CHUNK, HEAD_DIM)
    return (decay_in * scale,
            jnp.broadcast_to(zeta[:, :, None], lanes),
            jnp.broadcast_to((xi * scale)[:, :, None], lanes),
            jnp.broadcast_to(chunk_decay[:, None, None], (RET_HEADS, 1, HEAD_DIM)))


def _rotary(x, cos, sin):
    return x * cos + pltpu.roll(x, HEAD_DIM // 2, 1) * sin


def _chunks(x):
    return [x[r0:r0 + CHUNK] for r0 in range(0, x.shape[0], CHUNK)]


def _pool_row(u, halo_ref, j, ts):
    ue = jnp.concatenate([halo_ref[...], u], axis=0)
    halo_ref[...] = u[ts - POOL_HALO:]
    tpos = (lax.broadcasted_iota(jnp.int32, (ts, GROUP_CH), 0) + (j * ts + 1)).astype(F32)
    acc = ue
    width = 1
    pooled = []
    for gi, w in enumerate(POOL_WINDOWS):
        while width < w:
            acc = acc + pltpu.roll(acc, width, 0)
            width *= 2
        win = acc[POOL_HALO:, :GROUP_CH]
        acc = acc[:, GROUP_CH:]
        ug = u[:, gi * GROUP_CH:(gi + 1) * GROUP_CH]
        pooled.append((win / jnp.minimum(tpos, float(w)) - ug).astype(BF16))
    return pooled


def _mix_kernel(x_ref, cos_ref, sin_ref, g_ref, win_ref, pw_ref, ps_ref, wout_ref,
                dec_ref, zeta_ref, xi_ref, cd_ref, o_ref, state_ref, halo_ref, mix_ref):
    rows, ts, d = x_ref.shape
    j = pl.program_id(1)

    @pl.when(j == 0)
    def _():
        state_ref[...] = jnp.zeros_like(state_ref)
        halo_ref[...] = jnp.zeros_like(halo_ref)

    x = x_ref[...].reshape(rows * ts, d)
    cos = cos_ref[...].reshape(rows * ts, HEAD_DIM)
    sin = sin_ref[...].reshape(rows * ts, HEAD_DIM)
    h = _rmsnorm(x, g_ref[...]).astype(BF16)
    blk = 2 * HEAD_DIM
    z = {}

    def issue(*blocks):
        for i in blocks:
            z[i] = jnp.dot(h, win_ref[:, i * blk:(i + 1) * blk], preferred_element_type=F32)

    def head_cols(kind, hd):
        c0 = (hd % 2) * HEAD_DIM
        return z[2 * kind + hd // 2][:, c0:c0 + HEAD_DIM]

    def prep(hd):
        qs = _chunks(_rotary(head_cols(0, hd), cos, sin))
        ks = _chunks(_rotary(head_cols(1, hd), cos, sin))
        return ([q.astype(BF16) for q in qs], [(q * xi_ref[hd]).astype(BF16) for q in qs],
                [k.astype(BF16) for k in ks], [(k * zeta_ref[hd]).astype(BF16) for k in ks])

    nt_dims = (((1,), (1,)), ((), ()))
    tn_dims = (((0,), (0,)), ((), ()))
    n_chunk = ts // CHUNK
    pooled = []
    out = [x]

    def pool_maps():
        for gi in range(POOL_GROUPS):
            pg = jnp.concatenate([pooled[r][gi] for r in range(rows)], axis=0)
            yp = jnp.dot(pg, pw_ref[gi], preferred_element_type=F32) * ps_ref[:, gi * GROUP_CH:(gi + 1) * GROUP_CH]
            mix_ref[:, RET_WIDTH + gi * GROUP_CH:RET_WIDTH + (gi + 1) * GROUP_CH] = yp.astype(BF16)

    def out_rows(r0, r1):
        out[0] = out[0] + jnp.dot(mix_ref[:, r0:r1], wout_ref[r0:r1, :], preferred_element_type=F32)

    fill = {0: (lambda: issue(1), lambda: issue(3)), 1: (lambda: issue(5), lambda: issue(7)),
            2: (lambda: issue(8), lambda: issue(9)), 3: (pool_maps, lambda: out_rows(RET_WIDTH, d))}
    issue(0, 2)
    rot = {hd: prep(hd) for hd in (0, 1)}
    issue(4, 6)
    for hd in range(RET_HEADS):
        if hd == RET_HEADS - 1:
            u = jnp.concatenate([z[8], z[9]], axis=1)
            pooled.extend(_pool_row(u[r * ts:(r + 1) * ts], halo_ref.at[r], j, ts) for r in range(rows))
        dec, cd = dec_ref[hd], cd_ref[hd]
        qs, qxs, ks, kzs = rot[hd]
        vs = _chunks(head_cols(2, hd).astype(BF16))
        gs = _chunks(head_cols(3, hd))
        scores = [lax.dot_general(q, k, nt_dims, preferred_element_type=F32) for q, k in zip(qs, ks)]
        kvs = [lax.dot_general(kz, v, tn_dims, preferred_element_type=F32) for kz, v in zip(kzs, vs)]
        fill[hd][0]()
        ys = []
        for r in range(rows):
            st = state_ref[r, hd]
            for c in range(r * n_chunk, (r + 1) * n_chunk):
                lhs = jnp.concatenate([(scores[c] * dec).astype(BF16), qxs[c]], axis=1)
                rhs = jnp.concatenate([vs[c], st.astype(BF16)], axis=0)
                ys.append(jnp.dot(lhs, rhs, preferred_element_type=F32))
                st = st * cd + kvs[c]
            state_ref[r, hd] = st
        fill[hd][1]()
        if hd == 0:
            rot.update({n: prep(n) for n in (2, 3)})
        for c, (y, g) in enumerate(zip(ys, gs)):
            y = y * lax.rsqrt(jnp.mean(y * y, axis=-1, keepdims=True) + EPS)
            mix_ref[c * CHUNK:(c + 1) * CHUNK, hd * HEAD_DIM:(hd + 1) * HEAD_DIM] = (
                (g * jax.nn.sigmoid(g)) * y).astype(BF16)
    out_rows(0, RET_WIDTH)
    o_ref[...] = out[0].reshape(rows, ts, d)


def _const_spec(shape, layer, grid_rank):
    zeros = (0,) * len(shape)
    if grid_rank == 1:
        index_map = lambda i: (layer,) + zeros
    else:
        index_map = lambda i, j: (layer,) + zeros
    return pl.BlockSpec((None,) + tuple(shape), index_map, pipeline_mode=pl.Buffered(1))


def _table_spec(shape):
    return pl.BlockSpec(tuple(shape), lambda i, j: (0,) * len(shape), pipeline_mode=pl.Buffered(1))


def _mix_layer(x, cos, sin, tables, g_mix, w_in, pool_w, pool_scale, w_out, layer):
    b, s, d = x.shape
    ts = MIX_TILE
    rows = MIX_ROWS
    act = lambda width: pl.BlockSpec((rows, ts, width), lambda i, j: (i, j, 0))
    return pl.pallas_call(
        _mix_kernel,
        grid=(b // rows, s // ts),
        in_specs=[
            act(d), act(HEAD_DIM), act(HEAD_DIM),
            _const_spec((1, d), layer, 2),
            _const_spec((d, IN_WIDTH), layer, 2),
            _const_spec((POOL_GROUPS, GROUP_CH, GROUP_CH), layer, 2),
            _const_spec((1, POOL_WIDTH), layer, 2),
            _const_spec((d, d), layer, 2),
            _table_spec((RET_HEADS, CHUNK, CHUNK)),
            _table_spec((RET_HEADS, CHUNK, HEAD_DIM)),
            _table_spec((RET_HEADS, CHUNK, HEAD_DIM)),
            _table_spec((RET_HEADS, 1, HEAD_DIM)),
        ],
        out_specs=act(d),
        out_shape=jax.ShapeDtypeStruct(x.shape, x.dtype),
        scratch_shapes=[
            pltpu.VMEM((rows, RET_HEADS, HEAD_DIM, HEAD_DIM), F32),
            pltpu.VMEM((rows, POOL_HALO, POOL_WIDTH), F32),
            pltpu.VMEM((rows * ts, d), BF16),
        ],
        compiler_params=pltpu.CompilerParams(
            dimension_semantics=("arbitrary", "arbitrary"), vmem_limit_bytes=VMEM_LIMIT_BYTES),
        name=f"mix_layer{layer}",
    )(x, cos, sin, g_mix, w_in, pool_w, pool_scale, w_out, *tables)


def _ffn_kernel(x_ref, p_ref, gffn_ref, wg_ref, wu_ref, wd_ref, gple_ref, wpg_ref, wpp_ref, *rest, final):
    o_ref = rest[-1]
    x = x_ref[...]
    h = _rmsnorm(x, gffn_ref[...]).astype(BF16)
    c0 = 0
    for cw in FF_CHUNKS:
        gate = jnp.dot(h, wg_ref[:, c0:c0 + cw], preferred_element_type=F32)
        up = jnp.dot(h, wu_ref[:, c0:c0 + cw], preferred_element_type=F32)
        a = ((gate * jax.nn.sigmoid(gate)) * up).astype(BF16)
        x = x + jnp.dot(a, wd_ref[c0:c0 + cw, :], preferred_element_type=F32)
        c0 += cw
    hp = _rmsnorm(x, gple_ref[...]).astype(BF16)
    gate = jax.nn.sigmoid(jnp.dot(hp, wpg_ref[...], preferred_element_type=F32))
    e = jnp.dot(p_ref[...].astype(BF16), wpp_ref[...], preferred_element_type=F32)
    x = x + gate * e
    if final:
        x = _rmsnorm(x, rest[0][...])
    o_ref[...] = x


def _ffn_layer(x, p, g_ffn, w_gate, w_up, w_down, g_ple, w_ple_gate, w_ple_proj, g_final, layer):
    t, d = x.shape
    tm = FFN_TILE
    final = layer == DEPTH - 1
    in_specs = [
        pl.BlockSpec((tm, d), lambda i: (i, 0)),
        pl.BlockSpec((None, tm, PLE_DIM), lambda i: (layer, i, 0)),
        _const_spec((1, d), layer, 1),
        _const_spec((d, D_FF), layer, 1),
        _const_spec((d, D_FF), layer, 1),
        _const_spec((D_FF, d), layer, 1),
        _const_spec((1, d), layer, 1),
        _const_spec((d, d), layer, 1),
        _const_spec((PLE_DIM, d), layer, 1),
    ]
    args = [x, p, g_ffn, w_gate, w_up, w_down, g_ple, w_ple_gate, w_ple_proj]
    if final:
        in_specs.append(pl.BlockSpec((1, d), lambda i: (0, 0), pipeline_mode=pl.Buffered(1)))
        args.append(g_final)
    return pl.pallas_call(
        functools.partial(_ffn_kernel, final=final),
        grid=(t // tm,),
        in_specs=in_specs,
        out_specs=pl.BlockSpec((tm, d), lambda i: (i, 0)),
        out_shape=jax.ShapeDtypeStruct(x.shape, x.dtype),
        compiler_params=pltpu.CompilerParams(
            dimension_semantics=("arbitrary",), vmem_limit_bytes=VMEM_LIMIT_BYTES),
        name=f"ffn_layer{layer}",
    )(*args)


def kernel(x, p, positions, g_mix, w_in, pool_w, pool_scale, w_out, g_ffn, w_gate, w_up, w_down,
           g_ple, w_ple_gate, w_ple_proj, g_final):
    b, s, d = x.shape
    assert (d, s % MIX_TILE, b % MIX_ROWS, (b * s) % FFN_TILE) == (D_MODEL, 0, 0, 0)
    cos, sin = _rope_tables(positions)
    tables = _retention_tables()
    bf = lambda w: w.astype(BF16)
    w_in, pool_w, w_out, w_gate, w_up, w_down, w_ple_gate, w_ple_proj = map(
        bf, (w_in, pool_w, w_out, w_gate, w_up, w_down, w_ple_gate, w_ple_proj))
    row = lambda g: g.reshape(g.shape[0], 1, g.shape[1])
    g_mix, pool_scale, g_ffn, g_ple = map(row, (g_mix, pool_scale, g_ffn, g_ple))
    p = p.reshape(DEPTH, b * s, PLE_DIM)
    for layer in range(DEPTH):
        x = _mix_layer(x, cos, sin, tables, g_mix, w_in, pool_w, pool_scale, w_out, layer)
        x = _ffn_layer(x.reshape(b * s, d), p, g_ffn, w_gate, w_up, w_down, g_ple, w_ple_gate,
                       w_ple_proj, g_final.reshape(1, d), layer).reshape(b, s, d)
    return x
```

```python
import functools

import jax
import jax.numpy as jnp
from jax import lax
from jax.experimental import pallas as pl
from jax.experimental.pallas import tpu as pltpu

D_MODEL = 1024
DEPTH = 4
RET_HEADS = 4
HEAD_DIM = 128
RET_WIDTH = RET_HEADS * HEAD_DIM
POOL_WINDOWS = (2, 4, 8, 16)
POOL_GROUPS = len(POOL_WINDOWS)
GROUP_CH = 128
POOL_WIDTH = POOL_GROUPS * GROUP_CH
IN_WIDTH = 4 * RET_WIDTH + POOL_WIDTH
CHUNK = 128
ROPE_BASE = 10000.0
D_FF = 2816
PLE_DIM = 256
EPS = 1e-6

POOL_HALO = 16
MIX_TILE = 512
FFN_TILE = 1024
ROPE_TILE = 2048
MIX_ROWS = 2
FF_CHUNKS = (1024, 1024, 768)
VMEM_LIMIT_BYTES = 56 * 1024 * 1024

BF16_ROWS = 16

F32 = jnp.float32
BF16 = jnp.bfloat16


def _rmsnorm(x, g):
    ms = jnp.mean(x * x, axis=-1, keepdims=True)
    return (x * lax.rsqrt(ms + EPS)) * g


def _rope_kernel(pos_ref, inv_ref, cos_ref, sin_ref):
    half = HEAD_DIM // 2
    ang = pos_ref[...].astype(F32) * inv_ref[...]
    c = jnp.cos(ang)
    s = jnp.sin(ang)
    cr = pltpu.roll(c, half, 1)
    sr = pltpu.roll(s, half, 1)
    lo = lax.broadcasted_iota(jnp.int32, c.shape, 1) < half
    cos_ref[0] = jnp.where(lo, c, cr)
    cos_ref[1] = jnp.where(lo, cr, c)
    sin_ref[0] = jnp.where(lo, -s, sr)
    sin_ref[1] = jnp.where(lo, -sr, s)


def _rope_tables(positions):
    b, s = positions.shape
    half = HEAD_DIM // 2
    half_t = b * s // 2
    inv = 1.0 / (ROPE_BASE ** (jnp.arange(half, dtype=F32) * (2.0 / HEAD_DIM)))
    inv_full = jnp.concatenate([inv, inv])[None, :]
    flat = positions.reshape(b * s)
    pos = jnp.concatenate([jnp.broadcast_to(flat[:half_t, None], (half_t, half)),
                           jnp.broadcast_to(flat[half_t:, None], (half_t, half))], axis=1)
    tile = ROPE_TILE
    out_spec = pl.BlockSpec((2, tile, HEAD_DIM), lambda i: (0, i, 0))
    cos, sin = pl.pallas_call(
        _rope_kernel,
        grid=(half_t // tile,),
        in_specs=[pl.BlockSpec((tile, HEAD_DIM), lambda i: (i, 0)), pl.BlockSpec((1, HEAD_DIM), lambda i: (0, 0))],
        out_specs=[out_spec, out_spec],
        out_shape=[jax.ShapeDtypeStruct((2, half_t, HEAD_DIM), F32)] * 2,
        name="rope_tables",
    )(pos, inv_full)
    return cos.reshape(b, s, HEAD_DIM), sin.reshape(b, s, HEAD_DIM)


def _retention_tables():
    h = jnp.arange(RET_HEADS, dtype=F32)
    log_gamma = jnp.log1p(-(2.0 ** (-5.0 - h)))
    idx = jnp.arange(CHUNK, dtype=F32)
    rel = idx[:, None] - idx[None, :]
    decay_in = jnp.where(rel[None] >= 0,
                         jnp.exp(jnp.maximum(rel, 0.0)[None] * log_gamma[:, None, None]), 0.0)
    scale = HEAD_DIM ** -0.5
    zeta = jnp.exp((CHUNK - 1.0 - idx)[None, :] * log_gamma[:, None])
    xi = jnp.exp((idx + 1.0)[None, :] * log_gamma[:, None])
    chunk_decay = jnp.exp(CHUNK * log_gamma)
    lanes = (RET_HEADS, CHUNK, HEAD_DIM)
    return (decay_in * scale,
            jnp.broadcast_to(zeta[:, :, None], lanes),
            jnp.broadcast_to((xi * scale)[:, :, None], lanes),
            jnp.broadcast_to(chunk_decay[:, None, None], (RET_HEADS, 1, HEAD_DIM)))


def _fold_out_kernel(pw_ref, ps_ref, wout_ref, o_ref):
    o_ref[:RET_WIDTH, :] = wout_ref[:RET_WIDTH, :].astype(BF16)
    for gi in range(POOL_GROUPS):
        rows = slice(RET_WIDTH + gi * GROUP_CH, RET_WIDTH + (gi + 1) * GROUP_CH)
        wg = pw_ref[gi] * ps_ref[:, gi * GROUP_CH:(gi + 1) * GROUP_CH]
        o_ref[rows, :] = jnp.dot(wg, wout_ref[rows, :], precision=lax.Precision.HIGHEST,
                                 preferred_element_type=F32).astype(BF16)


def _fold_out_weights(pool_w, pool_scale, w_out):
    depth, d, _ = w_out.shape
    layer_block = lambda *shape: pl.BlockSpec((None,) + shape, lambda i: (i,) + (0,) * len(shape))
    return pl.pallas_call(
        _fold_out_kernel,
        grid=(depth,),
        in_specs=[layer_block(POOL_GROUPS, GROUP_CH, GROUP_CH), layer_block(1, POOL_WIDTH), layer_block(d, d)],
        out_specs=layer_block(d, d),
        out_shape=jax.ShapeDtypeStruct((depth, d, d), BF16),
        name="fold_out_weights",
    )(pool_w, pool_scale, w_out)


def _cast_plan(n_rows, steps):
    rb = -(-n_rows // steps)
    rb = -(-rb // BF16_ROWS) * BF16_ROWS
    while n_rows % rb:
        rb += BF16_ROWS
    return rb, n_rows // rb


def _rotary(x, cos, sin):
    return x * cos + pltpu.roll(x, HEAD_DIM // 2, 1) * sin


def _chunks(x):
    return [x[r0:r0 + CHUNK] for r0 in range(0, x.shape[0], CHUNK)]


def _pool_row(u, halo_ref, j, ts):
    ue = jnp.concatenate([halo_ref[...], u], axis=0)
    halo_ref[...] = u[ts - POOL_HALO:]
    tpos = (lax.broadcasted_iota(jnp.int32, (ts, GROUP_CH), 0) + (j * ts + 1)).astype(F32)
    acc = ue
    width = 1
    pooled = []
    for gi, w in enumerate(POOL_WINDOWS):
        while width < w:
            acc = acc + pltpu.roll(acc, width, 0)
            width *= 2
        win = acc[POOL_HALO:, :GROUP_CH]
        acc = acc[:, GROUP_CH:]
        ug = u[:, gi * GROUP_CH:(gi + 1) * GROUP_CH]
        pooled.append((win / jnp.minimum(tpos, float(w)) - ug).astype(BF16))
    return pooled


def _mix_kernel(x_ref, cos_ref, sin_ref, g_ref, win_ref, wout_ref,
                dec_ref, zeta_ref, xi_ref, cd_ref, o_ref, state_ref, halo_ref, mix_ref):
    rows, ts, d = x_ref.shape
    j = pl.program_id(1)

    @pl.when(j == 0)
    def _():
        state_ref[...] = jnp.zeros_like(state_ref)
        halo_ref[...] = jnp.zeros_like(halo_ref)

    x = x_ref[...].reshape(rows * ts, d)
    cos = cos_ref[...].reshape(rows * ts, HEAD_DIM)
    sin = sin_ref[...].reshape(rows * ts, HEAD_DIM)
    h = _rmsnorm(x, g_ref[...]).astype(BF16)
    blk = 2 * HEAD_DIM
    z = {}

    def issue(*blocks):
        for i in blocks:
            z[i] = jnp.dot(h, win_ref[:, i * blk:(i + 1) * blk], preferred_element_type=F32)

    def head_cols(kind, hd):
        c0 = (hd % 2) * HEAD_DIM
        return z[2 * kind + hd // 2][:, c0:c0 + HEAD_DIM]

    def prep(hd):
        qs = _chunks(_rotary(head_cols(0, hd), cos, sin))
        ks = _chunks(_rotary(head_cols(1, hd), cos, sin))
        return ([q.astype(BF16) for q in qs], [(q * xi_ref[hd]).astype(BF16) for q in qs],
                [k.astype(BF16) for k in ks], [(k * zeta_ref[hd]).astype(BF16) for k in ks])

    nt_dims = (((1,), (1,)), ((), ()))
    tn_dims = (((0,), (0,)), ((), ()))
    n_chunk = ts // CHUNK
    out = [x]

    def out_rows(r0, r1):
        out[0] = out[0] + jnp.dot(mix_ref[:, r0:r1], wout_ref[r0:r1, :], preferred_element_type=F32)

    fill = {0: (lambda: issue(1), lambda: issue(3)), 1: (lambda: issue(5), lambda: issue(7)),
            2: (lambda: issue(8), lambda: issue(9)), 3: (lambda: None, lambda: out_rows(RET_WIDTH, d))}
    issue(0, 2)
    rot = {hd: prep(hd) for hd in (0, 1)}
    issue(4, 6)
    for hd in range(RET_HEADS):
        if hd == RET_HEADS - 1:
            u = jnp.concatenate([z[8], z[9]], axis=1)
            for r in range(rows):
                for gi, pg in enumerate(_pool_row(u[r * ts:(r + 1) * ts], halo_ref.at[r], j, ts)):
                    mix_ref[r * ts:(r + 1) * ts, RET_WIDTH + gi * GROUP_CH:RET_WIDTH + (gi + 1) * GROUP_CH] = pg
        dec, cd = dec_ref[hd], cd_ref[hd]
        qs, qxs, ks, kzs = rot[hd]
        vs = _chunks(head_cols(2, hd).astype(BF16))
        gs = _chunks(head_cols(3, hd))
        scores = [lax.dot_general(q, k, nt_dims, preferred_element_type=F32) for q, k in zip(qs, ks)]
        kvs = [lax.dot_general(kz, v, tn_dims, preferred_element_type=F32) for kz, v in zip(kzs, vs)]
        fill[hd][0]()
        ys = []
        for r in range(rows):
            st = state_ref[r, hd]
            for c in range(r * n_chunk, (r + 1) * n_chunk):
                lhs = jnp.concatenate([(scores[c] * dec).astype(BF16), qxs[c]], axis=1)
                rhs = jnp.concatenate([vs[c], st.astype(BF16)], axis=0)
                ys.append(jnp.dot(lhs, rhs, preferred_element_type=F32))
                st = st * cd + kvs[c]
            state_ref[r, hd] = st
        fill[hd][1]()
        if hd == 0:
            rot.update({n: prep(n) for n in (2, 3)})
        for c, (y, g) in enumerate(zip(ys, gs)):
            y = y * lax.rsqrt(jnp.mean(y * y, axis=-1, keepdims=True) + EPS)
            mix_ref[c * CHUNK:(c + 1) * CHUNK, hd * HEAD_DIM:(hd + 1) * HEAD_DIM] = (
                (g * jax.nn.sigmoid(g)) * y).astype(BF16)
    out_rows(0, RET_WIDTH)
    o_ref[...] = out[0].reshape(rows, ts, d)


def _const_spec(shape, layer, grid_rank):
    zeros = (0,) * len(shape)
    if grid_rank == 1:
        index_map = lambda i: (layer,) + zeros
    else:
        index_map = lambda i, j: (layer,) + zeros
    return pl.BlockSpec((None,) + tuple(shape), index_map, pipeline_mode=pl.Buffered(1))


def _whole_spec(shape, grid_rank):
    zeros = (0,) * len(shape)
    index_map = (lambda i: zeros) if grid_rank == 1 else (lambda i, j: zeros)
    return pl.BlockSpec(tuple(shape), index_map, pipeline_mode=pl.Buffered(1))


def _mix_layer(x, cos, sin, tables, g_mix, w_in, w_out, layer):
    b, s, d = x.shape
    ts = MIX_TILE
    rows = MIX_ROWS
    act = lambda width: pl.BlockSpec((rows, ts, width), lambda i, j: (i, j, 0))
    return pl.pallas_call(
        _mix_kernel,
        grid=(b // rows, s // ts),
        in_specs=[
            act(d), act(HEAD_DIM), act(HEAD_DIM),
            _const_spec((1, d), layer, 2),
            _whole_spec((d, IN_WIDTH), 2),
            _const_spec((d, d), layer, 2),
            _whole_spec((RET_HEADS, CHUNK, CHUNK), 2),
            _whole_spec((RET_HEADS, CHUNK, HEAD_DIM), 2),
            _whole_spec((RET_HEADS, CHUNK, HEAD_DIM), 2),
            _whole_spec((RET_HEADS, 1, HEAD_DIM), 2),
        ],
        out_specs=act(d),
        out_shape=jax.ShapeDtypeStruct(x.shape, x.dtype),
        scratch_shapes=[
            pltpu.VMEM((rows, RET_HEADS, HEAD_DIM, HEAD_DIM), F32),
            pltpu.VMEM((rows, POOL_HALO, POOL_WIDTH), F32),
            pltpu.VMEM((rows * ts, d), BF16),
        ],
        compiler_params=pltpu.CompilerParams(
            dimension_semantics=("arbitrary", "arbitrary"), vmem_limit_bytes=VMEM_LIMIT_BYTES),
        name=f"mix_layer{layer}",
    )(x, cos, sin, g_mix, w_in, w_out, *tables)


def _ffn_kernel(x_ref, p_ref, gffn_ref, wg_ref, wu_ref, wd_ref, gple_ref, wpg_ref, wpp_ref, *rest, final, n_cast):
    n_in = len(rest) - 1 - n_cast
    o_ref = rest[n_in]
    for src, dst in zip(rest[n_in - n_cast:n_in], rest[n_in + 1:]):
        dst[...] = src[...].astype(BF16)
    x = x_ref[...]
    h = _rmsnorm(x, gffn_ref[...]).astype(BF16)
    c0 = 0
    for cw in FF_CHUNKS:
        gate = jnp.dot(h, wg_ref[:, c0:c0 + cw], preferred_element_type=F32)
        up = jnp.dot(h, wu_ref[:, c0:c0 + cw], preferred_element_type=F32)
        a = ((gate * jax.nn.sigmoid(gate)) * up).astype(BF16)
        x = x + jnp.dot(a, wd_ref[c0:c0 + cw, :], preferred_element_type=F32)
        c0 += cw
    hp = _rmsnorm(x, gple_ref[...]).astype(BF16)
    gate = jax.nn.sigmoid(jnp.dot(hp, wpg_ref[...], preferred_element_type=F32))
    e = jnp.dot(p_ref[...].astype(BF16), wpp_ref[...], preferred_element_type=F32)
    x = x + gate * e
    if final:
        x = _rmsnorm(x, rest[0][...])
    o_ref[...] = x


def _ffn_layer(x, p, g_ffn, g_ple, g_final, weights, next_f32, layer):
    t, d = x.shape
    tm = FFN_TILE
    steps = t // tm
    final = layer == DEPTH - 1
    in_specs = [
        pl.BlockSpec((tm, d), lambda i: (i, 0)),
        pl.BlockSpec((None, tm, PLE_DIM), lambda i: (layer, i, 0)),
        _const_spec((1, d), layer, 1),
        _whole_spec((d, D_FF), 1),
        _whole_spec((d, D_FF), 1),
        _whole_spec((D_FF, d), 1),
        _const_spec((1, d), layer, 1),
        _whole_spec((d, d), 1),
        _whole_spec((PLE_DIM, d), 1),
    ]
    args = [x, p, g_ffn, *weights[:3], g_ple, *weights[3:]]
    if final:
        in_specs.append(_whole_spec((1, d), 1))
        args.append(g_final)
    out_specs = [pl.BlockSpec((tm, d), lambda i: (i, 0))]
    out_shape = [jax.ShapeDtypeStruct(x.shape, x.dtype)]
    for w in next_f32:
        _, n_rows, n_cols = w.shape
        rb, n_blocks = _cast_plan(n_rows, steps)
        in_specs.append(pl.BlockSpec((None, rb, n_cols), lambda i, nb=n_blocks: (layer + 1, jnp.minimum(i, nb - 1), 0)))
        out_specs.append(pl.BlockSpec((rb, n_cols), lambda i, nb=n_blocks: (jnp.minimum(i, nb - 1), 0)))
        out_shape.append(jax.ShapeDtypeStruct((n_rows, n_cols), BF16))
        args.append(w)
    outs = pl.pallas_call(
        functools.partial(_ffn_kernel, final=final, n_cast=len(next_f32)),
        grid=(steps,),
        in_specs=in_specs,
        out_specs=out_specs,
        out_shape=out_shape,
        compiler_params=pltpu.CompilerParams(
            dimension_semantics=("arbitrary",), vmem_limit_bytes=VMEM_LIMIT_BYTES),
        name=f"ffn_layer{layer}",
    )(*args)
    return outs[0], tuple(outs[1:])


def kernel(x, p, positions, g_mix, w_in, pool_w, pool_scale, w_out, g_ffn, w_gate, w_up, w_down,
           g_ple, w_ple_gate, w_ple_proj, g_final):
    b, s, d = x.shape
    assert (d, s % MIX_TILE, b % MIX_ROWS, (b * s) % FFN_TILE, (b * s) % (2 * ROPE_TILE)) == (D_MODEL, 0, 0, 0, 0)
    cos, sin = _rope_tables(positions)
    tables = _retention_tables()
    row = lambda g: g.reshape(g.shape[0], 1, g.shape[1])
    g_mix, pool_scale, g_ffn, g_ple = map(row, (g_mix, pool_scale, g_ffn, g_ple))
    w_out = _fold_out_weights(pool_w, pool_scale, w_out)
    stacked = (w_in, w_gate, w_up, w_down, w_ple_gate, w_ple_proj)
    layer_w = tuple(w[0].astype(BF16) for w in stacked)
    p = p.reshape(DEPTH, b * s, PLE_DIM)
    for layer in range(DEPTH):
        x = _mix_layer(x, cos, sin, tables, g_mix, layer_w[0], w_out, layer)
        x, layer_w = _ffn_layer(x.reshape(b * s, d), p, g_ffn, g_ple, g_final.reshape(1, d), layer_w[1:],
                                stacked if layer + 1 < DEPTH else (), layer)
        x = x.reshape(b, s, d)
    return x
```
